```python
import math
import jax, jax.numpy as jnp
from jax import lax
import numpy as np

D_MODEL = 1024
BATCH = 2
SEQ = 8192
DEPTH = 1
DEC_BATCH = 128
DEC_SEQ = 8
PAST_LEN = 8192
PAGE_SIZE = 128

N_META = 16
EPS = 1e-6
ROPE_THETA = 10000.0
MLA_HEADS = 8
Q_LORA = 256
KV_LORA = 128
NOPE_DIM = 64
ROPE_DIM = 32
V_DIM = 64
MLA_SCALE = 1.0 / math.sqrt(NOPE_DIM + ROPE_DIM)
Q_BLOCK = 128
RET_HEADS = 8
RET_DK = 64
RET_DV = 64
RET_CHUNK = 128
MIX_WIDTH = MLA_HEADS * V_DIM + RET_HEADS * RET_DV
IN_WIDTH = Q_LORA + KV_LORA + ROPE_DIM + 3 * RET_HEADS * RET_DK + RET_HEADS * RET_DV
PEER_HEADS = 8
N_KEYS = 128
N_EXPERTS = N_KEYS * N_KEYS
PEER_KEY_DIM = 128
PEER_TOPK = 16
PEER_BLOCK = 128

kernel_name = 'hymba_mla_retention_peer_step'


def rmsnorm(x, g):
    xf = x.astype(jnp.float32)
    y = xf * lax.rsqrt(jnp.mean(xf * xf, axis=-1, keepdims=True) + EPS)
    return (y * g.astype(jnp.float32)).astype(x.dtype)


def head_norm(x):
    xf = x.astype(jnp.float32)
    y = xf * lax.rsqrt(jnp.mean(xf * xf, axis=-1, keepdims=True) + EPS)
    return y.astype(x.dtype)


def rope(x, pos):
    half = x.shape[-1] // 2
    inv = ROPE_THETA ** (-jnp.arange(half, dtype=jnp.float32) / half)
    ang = pos.astype(jnp.float32)[:, None] * inv[None, :]
    shp = (pos.shape[0],) + (1,) * (x.ndim - 3) + (half,)
    cos = jnp.cos(ang).reshape(shp).astype(x.dtype)
    sin = jnp.sin(ang).reshape(shp).astype(x.dtype)
    x1, x2 = x[..., :half], x[..., half:]
    return jnp.concatenate([x1 * cos - x2 * sin, x1 * sin + x2 * cos], axis=-1)


def in_split_points():
    widths = (Q_LORA, KV_LORA, ROPE_DIM, RET_HEADS * RET_DK, RET_HEADS * RET_DK, RET_HEADS * RET_DV)
    pts, acc = [], 0
    for w in widths:
        acc += w
        pts.append(acc)
    return pts


def mixer_inputs(h, pos, w_in, g_q, w_uq, g_kv, w_uk):
    b, l, _ = h.shape
    c_q, c_kv, k_r, r_q, r_k, r_v, r_g = jnp.split(h @ w_in, in_split_points(), axis=-1)
    q = jnp.einsum('blc,chd->blhd', rmsnorm(c_q, g_q), w_uq)
    q_rope = rope(q[..., NOPE_DIM:], pos)
    q_lat = jnp.einsum('blhd,chd->blhc', q[..., :NOPE_DIM], w_uk)
    ckv = rmsnorm(c_kv, g_kv)
    krope = rope(k_r, pos)
    rq = rope(r_q.reshape(b, l, RET_HEADS, RET_DK), pos)
    rk = rope(r_k.reshape(b, l, RET_HEADS, RET_DK), pos) * (RET_DK ** -0.5)
    rv = r_v.reshape(b, l, RET_HEADS, RET_DV)
    return q_lat, q_rope, ckv, krope, rq, rk, rv, r_g


def mla_attend(q_lat, q_rope, ckv, krope, q_pos, k_pos):
    s = jnp.einsum('bqhc,bkc->bhqk', q_lat, ckv) + jnp.einsum('bqhr,bkr->bhqk', q_rope, krope)
    s = s.astype(jnp.float32) * MLA_SCALE
    s = jnp.where(k_pos[None, :] <= q_pos[:, None], s, -jnp.inf)
    p = jax.nn.softmax(s, axis=-1).astype(ckv.dtype)
    return jnp.einsum('bhqk,bkc->bqhc', p, ckv)


def mla_prompt(q_lat, q_rope, ckv, krope):
    b, l = q_lat.shape[:2]
    nb = -(-l // Q_BLOCK)
    lp = nb * Q_BLOCK
    pad_len = lambda a: jnp.pad(a, [(0, 0), (0, lp - l)] + [(0, 0)] * (a.ndim - 2))
    q_lat, q_rope, ckv, krope = pad_len(q_lat), pad_len(q_rope), pad_len(ckv), pad_len(krope)
    k_pos = jnp.arange(lp)

    def to_blocks(a):
        return jnp.moveaxis(a.reshape((b, nb, Q_BLOCK) + a.shape[2:]), 1, 0)

    def blk(args):
        qb, qrb, qp = args
        return mla_attend(qb, qrb, ckv, krope, qp, k_pos)

    out = lax.map(blk, (to_blocks(q_lat), to_blocks(q_rope), k_pos.reshape(nb, Q_BLOCK)))
    out = jnp.moveaxis(out, 0, 1).reshape((b, lp) + out.shape[3:])
    return out[:, :l]


def ret_chunk(s, q, k, v):
    c = q.shape[1]
    log_g = jnp.log(1.0 - 2.0 ** (-5.0 - jnp.arange(RET_HEADS, dtype=jnp.float32)))
    i = jnp.arange(c, dtype=jnp.float32)
    rel = i[:, None] - i[None, :]
    dmat = jnp.where(rel >= 0, jnp.exp(log_g[:, None, None] * jnp.maximum(rel, 0.0)), 0.0).astype(q.dtype)
    inner = jnp.einsum('bihd,bjhd->bhij', q, k) * dmat
    o = jnp.einsum('bhij,bjhe->bihe', inner, v)
    cross = jnp.exp(log_g[None, :] * (i[:, None] + 1.0)).astype(q.dtype)
    o = o + jnp.einsum('bihd,bhde->bihe', q, s) * cross[None, :, :, None]
    kdec = jnp.exp(log_g[None, :] * (c - 1.0 - i)[:, None]).astype(q.dtype)
    s_new = s * jnp.exp(log_g * c).astype(s.dtype)[None, :, None, None] \
        + jnp.einsum('bjhd,bjhe->bhde', k * kdec[None, :, :, None], v).astype(s.dtype)
    return s_new, o


def retention_prompt(q, k, v):
    b = q.shape[0]
    s0 = jnp.zeros((b, RET_HEADS, RET_DK, RET_DV), q.dtype)
    s_meta, o_meta = ret_chunk(s0, q[:, :N_META], k[:, :N_META], v[:, :N_META])
    nc = (q.shape[1] - N_META) // RET_CHUNK

    def chunks(a):
        return jnp.moveaxis(a[:, N_META:].reshape(b, nc, RET_CHUNK, RET_HEADS, a.shape[-1]), 1, 0)

    s_fin, o = lax.scan(lambda st, qkv: ret_chunk(st, qkv[0], qkv[1], qkv[2]), s_meta,
                        (chunks(q), chunks(k), chunks(v)))
    o = jnp.moveaxis(o, 0, 1).reshape(b, nc * RET_CHUNK, RET_HEADS, RET_DV)
    return s_fin, jnp.concatenate([o_meta, o], axis=1)


def mixer_output(lat, ro, r_g, w_uv, w_out):
    b, l = lat.shape[:2]
    mla = jnp.einsum('blhc,chd->blhd', lat, w_uv).reshape(b, l, MLA_HEADS * V_DIM)
    ret = head_norm(ro).reshape(b, l, RET_HEADS * RET_DV) * jax.nn.silu(r_g)
    return jnp.concatenate([mla, ret], axis=-1) @ w_out


def peer_ffn(h, w_q, sub_keys, u_tab, v_tab):
    b, l, d = h.shape
    t = b * l
    nb = -(-t // PEER_BLOCK)
    hf = jnp.pad(h.reshape(t, d), ((0, nb * PEER_BLOCK - t), (0, 0))).reshape(nb, PEER_BLOCK, d)

    def blk(hb):
        q = (hb @ w_q).reshape(PEER_BLOCK, PEER_HEADS, 2, PEER_KEY_DIM // 2)
        sc = jnp.einsum('thpd,hpnd->thpn', q, sub_keys).astype(jnp.float32)
        s1, i1 = lax.top_k(sc[:, :, 0], PEER_TOPK)
        s2, i2 = lax.top_k(sc[:, :, 1], PEER_TOPK)
        cand = (s1[..., :, None] + s2[..., None, :]).reshape(PEER_BLOCK, PEER_HEADS, PEER_TOPK * PEER_TOPK)
        top_s, ci = lax.top_k(cand, PEER_TOPK)
        e = jnp.take_along_axis(i1, ci // PEER_TOPK, axis=-1) * N_KEYS \
            + jnp.take_along_axis(i2, ci % PEER_TOPK, axis=-1)
        gate = jax.nn.softmax(top_s, axis=-1).astype(hb.dtype)
        act = jax.nn.gelu(jnp.einsum('thkd,td->thk', u_tab[e], hb), approximate=False)
        return jnp.einsum('thk,thkd->td', gate * act, v_tab[e])

    out = lax.map(blk, hf).reshape(nb * PEER_BLOCK, d)[:t]
    return out.reshape(b, l, d)


def setup_inputs(seed: int = 0) -> dict:
    key = jax.random.key(seed)
    ks = jax.random.split(key, 24)
    n_pages = PAST_LEN // PAGE_SIZE
    n_used = DEC_BATCH * n_pages
    n_pool = (n_used * 5) // 4
    nrm = lambda k, shp, sc: jax.random.normal(k, shp, jnp.float32) * sc
    gain = lambda k, n: 1.0 + 0.02 * jax.random.normal(k, (DEPTH, n), jnp.float32)
    page_table = jax.random.permutation(ks[5], n_pool)[:n_used].astype(jnp.int32).reshape(DEC_BATCH, n_pages)
    return {
        'x_prompt': nrm(ks[0], (BATCH, SEQ, D_MODEL), 1.0),
        'x_sample': nrm(ks[1], (DEC_BATCH, DEC_SEQ, D_MODEL), 1.0),
        'cache_ckv': nrm(ks[2], (DEPTH, n_pool, PAGE_SIZE, KV_LORA), 1.0),
        'cache_krope': nrm(ks[3], (DEPTH, n_pool, PAGE_SIZE, ROPE_DIM), 1.0),
        'state_ret': nrm(ks[4], (DEPTH, DEC_BATCH, RET_HEADS, RET_DK, RET_DV), 0.1),
        'page_table': page_table,
        'meta_tokens': nrm(ks[6], (N_META, D_MODEL), 1.0),
        'g_mix': gain(ks[7], D_MODEL),
        'w_in': nrm(ks[8], (DEPTH, D_MODEL, IN_WIDTH), D_MODEL ** -0.5),
        'g_q': gain(ks[9], Q_LORA),
        'w_uq': nrm(ks[10], (DEPTH, Q_LORA, MLA_HEADS, NOPE_DIM + ROPE_DIM), Q_LORA ** -0.5),
        'g_kv': gain(ks[11], KV_LORA),
        'w_uk': nrm(ks[12], (DEPTH, KV_LORA, MLA_HEADS, NOPE_DIM), KV_LORA ** -0.5),
        'w_uv': nrm(ks[13], (DEPTH, KV_LORA, MLA_HEADS, V_DIM), KV_LORA ** -0.5),
        'w_out': nrm(ks[14], (DEPTH, MIX_WIDTH, D_MODEL), MIX_WIDTH ** -0.5),
        'g_ffn': gain(ks[15], D_MODEL),
        'w_query': nrm(ks[16], (DEPTH, D_MODEL, PEER_HEADS * PEER_KEY_DIM), D_MODEL ** -0.5),
        'sub_keys': nrm(ks[17], (DEPTH, PEER_HEADS, 2, N_KEYS, PEER_KEY_DIM // 2), (PEER_KEY_DIM // 2) ** -0.5),
        'u_experts': nrm(ks[18], (DEPTH, N_EXPERTS, D_MODEL), D_MODEL ** -0.5),
        'v_experts': nrm(ks[19], (DEPTH, N_EXPERTS, D_MODEL), 0.5),
        'g_final': 1.0 + 0.02 * jax.random.normal(ks[20], (D_MODEL,), jnp.float32),
    }


def reference(x_prompt, x_sample, cache_ckv, cache_krope, state_ret, page_table, meta_tokens,
              g_mix, w_in, g_q, w_uq, g_kv, w_uk, w_uv, w_out, g_ffn, w_query, sub_keys,
              u_experts, v_experts, g_final):
    b = x_prompt.shape[0]
    db, n_pages = page_table.shape
    past = n_pages * PAGE_SIZE
    dec = x_sample.shape[1]
    lp = N_META + x_prompt.shape[1]
    pos_p = jnp.arange(lp)
    pos_s = past + jnp.arange(dec)
    k_pos_s = jnp.arange(past + dec)

    meta = jnp.broadcast_to(meta_tokens.astype(x_prompt.dtype)[None], (b, N_META, D_MODEL))
    xp = jnp.concatenate([meta, x_prompt], axis=1)
    xs = x_sample
    ckv_p_l, kr_p_l, ckv_s_l, kr_s_l, rs_p_l, rs_s_l = [], [], [], [], [], []

    for l in range(DEPTH):
        q_lat, q_rope, ckv, krope, rq, rk, rv, rg = mixer_inputs(
            rmsnorm(xp, g_mix[l]), pos_p, w_in[l], g_q[l], w_uq[l], g_kv[l], w_uk[l])
        lat = mla_prompt(q_lat, q_rope, ckv, krope)
        s_p, ro = retention_prompt(rq, rk, rv)
        xp = xp + mixer_output(lat, ro, rg, w_uv[l], w_out[l])
        xp = xp + peer_ffn(rmsnorm(xp, g_ffn[l]), w_query[l], sub_keys[l], u_experts[l], v_experts[l])
        ckv_p_l.append(ckv)
        kr_p_l.append(krope)
        rs_p_l.append(s_p)

        q_lat_s, q_rope_s, ckv_s, krope_s, rq_s, rk_s, rv_s, rg_s = mixer_inputs(
            rmsnorm(xs, g_mix[l]), pos_s, w_in[l], g_q[l], w_uq[l], g_kv[l], w_uk[l])
        ckv_all = jnp.concatenate(
            [cache_ckv[l][page_table].reshape(db, past, KV_LORA).astype(ckv_s.dtype), ckv_s], axis=1)
        kr_all = jnp.concatenate(
            [cache_krope[l][page_table].reshape(db, past, ROPE_DIM).astype(krope_s.dtype), krope_s], axis=1)
        lat_s = mla_attend(q_lat_s, q_rope_s, ckv_all, kr_all, pos_s, k_pos_s)
        s_s, ro_s = ret_chunk(state_ret[l].astype(rq_s.dtype), rq_s, rk_s, rv_s)
        xs = xs + mixer_output(lat_s, ro_s, rg_s, w_uv[l], w_out[l])
        xs = xs + peer_ffn(rmsnorm(xs, g_ffn[l]), w_query[l], sub_keys[l], u_experts[l], v_experts[l])
        ckv_s_l.append(ckv_s)
        kr_s_l.append(krope_s)
        rs_s_l.append(s_s)

    y_prompt = rmsnorm(xp, g_final)[:, N_META:]
    y_sample = rmsnorm(xs, g_final)
    new_ckv_prompt = jnp.stack(ckv_p_l, axis=0)
    new_krope_prompt = jnp.stack(kr_p_l, axis=0)
    new_ckv_sample = jnp.stack(ckv_s_l, axis=0)
    new_krope_sample = jnp.stack(kr_s_l, axis=0)
    new_ret_prompt = jnp.stack(rs_p_l, axis=0)
    new_ret_sample = jnp.stack(rs_s_l, axis=0)
    return (y_prompt, y_sample, new_ckv_prompt, new_krope_prompt, new_ckv_sample, new_krope_sample,
            new_ret_prompt, new_ret_sample)
```

```python
import functools
import math

import numpy as np
import jax
import jax.numpy as jnp
from jax import lax
from jax.experimental import pallas as pl
from jax.experimental.pallas import tpu as pltpu

N_META = 16
EPS = 1e-6
ROPE_THETA = 10000.0
MLA_HEADS = 8
Q_LORA = 256
KV_LORA = 128
NOPE_DIM = 64
ROPE_DIM = 32
V_DIM = 64
MLA_SCALE = 1.0 / math.sqrt(NOPE_DIM + ROPE_DIM)
RET_HEADS = 8
RET_DK = 64
RET_DV = 64
PEER_HEADS = 8
N_KEYS = 128
PEER_KEY_DIM = 128
PEER_TOPK = 16
PAGE_SIZE = 128

LANES = 128
QK_WIDTH = 2 * LANES
RET_CHUNK = 128
TOKEN_BLOCK = 256
ATTN_TILE = 256
PEER_TOKEN_BLOCK = 512
PEER_EXPERT_BLOCK = 512
SAMPLE_PAGES_PER_STEP = 16
SAMPLE_RET_BATCH = 16
VMEM_LIMIT = 48 * 1024 * 1024

_F32 = jnp.float32
_BF16 = jnp.bfloat16
_NEG_INF = float("-inf")


def _dot(a, b):
    return jnp.dot(a, b, preferred_element_type=_F32)


def _dot_nt(a, b):
    return lax.dot_general(a, b, (((1,), (1,)), ((), ())), preferred_element_type=_F32)


def _dot_tn(a, b):
    return lax.dot_general(a, b, (((0,), (0,)), ((), ())), preferred_element_type=_F32)


def _rms(x):
    return x * lax.rsqrt(jnp.mean(x * x, axis=-1, keepdims=True) + EPS)


def _rope(x, cos, sin_signed, group):
    width = x.shape[1]
    half = group // 2
    reps = width // LANES
    if reps > 1:
        cos = jnp.concatenate([cos] * reps, axis=1)
        sin_signed = jnp.concatenate([sin_signed] * reps, axis=1)
    lane = lax.broadcasted_iota(jnp.int32, x.shape, 1)
    first = (lane & (group - 1)) < half
    swapped = jnp.where(first, pltpu.roll(x, width - half, 1), pltpu.roll(x, half, 1))
    return x * cos + swapped * sin_signed


def _inproj_kernel(x_ref, c32_ref, s32_ref, c64_ref, s64_ref, gmix_ref, w1_ref, w2_ref, gq_ref,
                   wqn_ref, wqr_ref, wuk_ref, gkv_ref,
                   qcat_ref, kcat_ref, ckv_ref, krope_ref, rq_ref, rk_ref, rv_ref, sg_ref):
    x = x_ref[...]
    h = (_rms(x) * gmix_ref[...]).astype(_BF16)
    p1 = _dot(h, w1_ref[...])
    p2 = _dot(h, w2_ref[...])
    c32, s32 = c32_ref[...], s32_ref[...]
    c64, s64 = c64_ref[...], s64_ref[...]

    cqn = (_rms(p1[:, :Q_LORA]) * gq_ref[...]).astype(_BF16)
    q_nope = _dot(cqn, wqn_ref[...]).astype(_BF16)
    q_lat = _dot(q_nope, wuk_ref[...])
    q_rope = _rope(_dot(cqn, wqr_ref[...]), c32, s32, ROPE_DIM)
    for hd in range(MLA_HEADS):
        sl = slice(hd * LANES, (hd + 1) * LANES)
        qcat_ref[hd, :, :LANES] = (q_lat[:, sl] * MLA_SCALE).astype(_BF16)
        qcat_ref[hd, :, LANES:] = (q_rope[:, sl] * MLA_SCALE).astype(_BF16)

    ckv = _rms(p1[:, Q_LORA:Q_LORA + KV_LORA]) * gkv_ref[...]
    ckv_ref[...] = ckv
    kcat_ref[:, :LANES] = ckv.astype(_BF16)
    kr = _rope(p1[:, Q_LORA + KV_LORA:], c32, s32, ROPE_DIM)
    krope_ref[...] = kr[:, :ROPE_DIM]
    kcat_ref[:, LANES:] = kr.astype(_BF16)

    w = RET_HEADS * RET_DK
    rq_ref[...] = _rope(p2[:, :w], c64, s64, RET_DK)
    rk_ref[...] = _rope(p2[:, w:2 * w], c64, s64, RET_DK) * (RET_DK ** -0.5)
    rv_ref[...] = p2[:, 2 * w:3 * w]
    rg = p2[:, 3 * w:]
    sg_ref[...] = rg * (1.0 / (1.0 + jnp.exp(-rg)))


def _inproj(x_all, tabs, gmix, w1, w2, gq, wqn, wqr, wuk, gkv):
    tt = x_all.shape[0]
    tb = TOKEN_BLOCK
    d = x_all.shape[1]
    row = lambda width: pl.BlockSpec((tb, width), lambda i: (i, 0))
    full = lambda a: pl.BlockSpec(a.shape, lambda i: (0,) * a.ndim)
    w = RET_HEADS * RET_DK
    out_shape = (
        jax.ShapeDtypeStruct((MLA_HEADS, tt, QK_WIDTH), _BF16),
        jax.ShapeDtypeStruct((tt, QK_WIDTH), _BF16),
        jax.ShapeDtypeStruct((tt, KV_LORA), _F32),
        jax.ShapeDtypeStruct((tt, ROPE_DIM), _F32),
        jax.ShapeDtypeStruct((tt, w), _F32),
        jax.ShapeDtypeStruct((tt, w), _F32),
        jax.ShapeDtypeStruct((tt, w), _F32),
        jax.ShapeDtypeStruct((tt, w), _F32),
    )
    out_specs = (
        pl.BlockSpec((MLA_HEADS, tb, QK_WIDTH), lambda i: (0, i, 0)),
        row(QK_WIDTH), row(KV_LORA), row(ROPE_DIM), row(w), row(w), row(w), row(w),
    )
    return pl.pallas_call(
        _inproj_kernel,
        grid=(tt // tb,),
        in_specs=[row(d), row(LANES), row(LANES), row(LANES), row(LANES), full(gmix), full(w1), full(w2),
                  full(gq), full(wqn), full(wqr), full(wuk), full(gkv)],
        out_specs=out_specs,
        out_shape=out_shape,
        compiler_params=pltpu.CompilerParams(dimension_semantics=("parallel",), vmem_limit_bytes=VMEM_LIMIT),
        name="inproj",
    )(x_all, *tabs, gmix, w1, w2, gq, wqn, wqr, wuk, gkv)


def _attn_prompt_kernel(q_ref, k_ref, o_ref, m_ref, l_ref, acc_ref):
    i = pl.program_id(1)
    t = ATTN_TILE
    rows = MLA_HEADS * t
    q = q_ref[...].reshape(rows, QK_WIDTH)
    m_ref[...] = jnp.full((rows, 1), _NEG_INF, _F32)
    l_ref[...] = jnp.zeros((rows, 1), _F32)
    acc_ref[...] = jnp.zeros((rows, KV_LORA), _F32)

    def step(kblk, causal):
        s = _dot_nt(q, kblk)
        if causal:
            qpos = lax.broadcasted_iota(jnp.int32, (rows, t), 0) & (t - 1)
            kpos = lax.broadcasted_iota(jnp.int32, (rows, t), 1)
            s = jnp.where(kpos <= qpos, s, _NEG_INF)
        m_prev = m_ref[...]
        m_new = jnp.maximum(m_prev, jnp.max(s, axis=1, keepdims=True))
        alpha = jnp.exp(m_prev - m_new)
        p = jnp.exp(s - m_new)
        l_ref[...] = alpha * l_ref[...] + jnp.sum(p, axis=1, keepdims=True)
        acc_ref[...] = alpha * acc_ref[...] + _dot(p.astype(_BF16), kblk[:, :KV_LORA])
        m_ref[...] = m_new

    def body(j, carry):
        step(k_ref[pl.ds(pl.multiple_of(j * t, t), t), :], False)
        return carry

    lax.fori_loop(0, i, body, 0)
    step(k_ref[pl.ds(pl.multiple_of(i * t, t), t), :], True)

    out = acc_ref[...] / l_ref[...]
    for hd in range(MLA_HEADS):
        o_ref[:, hd * KV_LORA:(hd + 1) * KV_LORA] = out[hd * t:(hd + 1) * t].astype(o_ref.dtype)


def _attn_prompt(qcat, kcat, n_batch, lp):
    t = ATTN_TILE
    nq = lp // t
    rows = MLA_HEADS * t
    return pl.pallas_call(
        _attn_prompt_kernel,
        grid=(n_batch, nq),
        in_specs=[pl.BlockSpec((MLA_HEADS, t, QK_WIDTH), lambda b, i: (0, b * nq + i, 0)),
                  pl.BlockSpec((lp, QK_WIDTH), lambda b, i: (b, 0))],
        out_specs=pl.BlockSpec((t, MLA_HEADS * KV_LORA), lambda b, i: (b * nq + i, 0)),
        out_shape=jax.ShapeDtypeStruct((n_batch * lp, MLA_HEADS * KV_LORA), _BF16),
        scratch_shapes=[pltpu.VMEM((rows, 1), _F32), pltpu.VMEM((rows, 1), _F32),
                        pltpu.VMEM((rows, KV_LORA), _F32)],
        compiler_params=pltpu.CompilerParams(dimension_semantics=("parallel", "parallel"),
                                             vmem_limit_bytes=VMEM_LIMIT),
        name="attn_prompt",
    )(qcat, kcat)


def _ret_prompt_kernel(q_ref, k_ref, v_ref, dmat_ref, cross_ref, kdec_ref, sdec_ref,
                       o_ref, sfin_ref, s_ref, *, last_chunk):
    c = pl.program_id(1)

    @pl.when(c == 0)
    def _():
        s_ref[...] = jnp.zeros_like(s_ref)

    for hd in range(RET_HEADS):
        sl = slice(hd * RET_DK, (hd + 1) * RET_DK)
        qf, kf, vf = q_ref[:, sl], k_ref[:, sl], v_ref[:, sl]
        q, v = qf.astype(_BF16), vf.astype(_BF16)
        inner = _dot_nt(q, kf.astype(_BF16)) * dmat_ref[hd]
        s = s_ref[hd]
        o = _dot(inner.astype(_BF16), v) + _dot(q, s.astype(_BF16)) * cross_ref[hd]
        o_ref[:, sl] = o * lax.rsqrt(jnp.mean(o * o, axis=-1, keepdims=True) + EPS)
        kd = (kf * kdec_ref[0, hd]).astype(_BF16)
        s_ref[hd] = s * sdec_ref[0, hd] + _dot_tn(kd, v)

    @pl.when(c == last_chunk)
    def _():
        sfin_ref[0] = s_ref[...]


def _ret_tables(chunk, rem):
    log_g = jnp.log(1.0 - 2.0 ** (-5.0 - jnp.arange(RET_HEADS, dtype=_F32)))[:, None, None]
    i = jnp.arange(chunk, dtype=_F32)
    rel = i[:, None] - i[None, :]
    dmat = jnp.where(rel >= 0, jnp.exp(log_g * jnp.maximum(rel, 0.0)), 0.0)
    cross = jnp.broadcast_to(jnp.exp(log_g * (i[None, :, None] + 1.0)), (RET_HEADS, chunk, RET_DV))
    kfull = jnp.exp(log_g * (chunk - 1.0 - i)[None, :, None])
    klast = jnp.where(i[None, :, None] < rem, jnp.exp(log_g * jnp.maximum(rem - 1.0 - i, 0.0)[None, :, None]), 0.0)
    kdec = jnp.broadcast_to(jnp.stack([kfull, klast]), (2, RET_HEADS, chunk, RET_DK))
    sdec = jnp.broadcast_to(jnp.stack([jnp.exp(log_g * float(chunk)), jnp.exp(log_g * float(rem))]),
                            (2, RET_HEADS, RET_DK, RET_DV))
    return dmat.astype(_F32), cross.astype(_F32), kdec.astype(_F32), sdec.astype(_F32)


def _ret_prompt(rq, rk, rv, n_batch, lp, l_real):
    ch = RET_CHUNK
    nc = lp // ch
    n_full, rem = divmod(l_real, ch)
    last_chunk = n_full if rem else n_full - 1
    dmat, cross, kdec, sdec = _ret_tables(ch, rem)
    w = RET_HEADS * RET_DK
    row = pl.BlockSpec((ch, w), lambda b, c: (b * nc + c, 0))
    if rem:
        tab_sel = lambda b, c: (jnp.where(c == n_full, 1, 0), 0, 0, 0)
    else:
        tab_sel = lambda b, c: (0, 0, 0, 0)
    return pl.pallas_call(
        functools.partial(_ret_prompt_kernel, last_chunk=last_chunk),
        grid=(n_batch, nc),
        in_specs=[row, row, row,
                  pl.BlockSpec(dmat.shape, lambda b, c: (0, 0, 0)),
                  pl.BlockSpec(cross.shape, lambda b, c: (0, 0, 0)),
                  pl.BlockSpec((1,) + kdec.shape[1:], tab_sel),
                  pl.BlockSpec((1,) + sdec.shape[1:], tab_sel)],
        out_specs=(row, pl.BlockSpec((1, RET_HEADS, RET_DK, RET_DV), lambda b, c: (b, 0, 0, 0))),
        out_shape=(jax.ShapeDtypeStruct((n_batch * lp, w), _F32),
                   jax.ShapeDtypeStruct((n_batch, RET_HEADS, RET_DK, RET_DV), _F32)),
        scratch_shapes=[pltpu.VMEM((RET_HEADS, RET_DK, RET_DV), _F32)],
        compiler_params=pltpu.CompilerParams(dimension_semantics=("parallel", "arbitrary"),
                                             vmem_limit_bytes=VMEM_LIMIT),
        name="ret_prompt",
    )(rq, rk, rv, dmat, cross, kdec, sdec)


def _ret_sample_kernel(q_ref, k_ref, v_ref, s_ref, dblk_ref, cross_ref, kdec_ref, sdec_ref,
                       o_ref, snew_ref, *, dec):
    nb = SAMPLE_RET_BATCH
    for hd in range(RET_HEADS):
        sl = slice(hd * RET_DK, (hd + 1) * RET_DK)
        qf, kf, vf = q_ref[:, sl], k_ref[:, sl], v_ref[:, sl]
        q, v = qf.astype(_BF16), vf.astype(_BF16)
        inner = _dot_nt(q, kf.astype(_BF16)) * dblk_ref[hd]
        o_intra = _dot(inner.astype(_BF16), v)
        kd = kf * kdec_ref[hd]
        for b in range(nb):
            rs = slice(b * dec, (b + 1) * dec)
            s = s_ref[b, hd]
            o = o_intra[rs] + _dot(qf[rs].astype(_BF16), s.astype(_BF16)) * cross_ref[hd]
            o_ref[rs, sl] = o * lax.rsqrt(jnp.mean(o * o, axis=-1, keepdims=True) + EPS)
            snew_ref[b, hd] = s * sdec_ref[hd] + _dot_tn(kd[rs].astype(_BF16), vf[rs].astype(_BF16))


def _ret_sample(rq, rk, rv, state, row0, db, dec):
    nb = SAMPLE_RET_BATCH
    rows = nb * dec
    log_g = jnp.log(1.0 - 2.0 ** (-5.0 - jnp.arange(RET_HEADS, dtype=_F32)))[:, None, None]
    i = jnp.arange(dec, dtype=_F32)
    rel = i[:, None] - i[None, :]
    d_small = jnp.where(rel >= 0, jnp.exp(log_g * jnp.maximum(rel, 0.0)), 0.0)
    eye = jnp.eye(nb, dtype=_F32)
    dblk = jnp.einsum("hij,ab->haibj", d_small, eye).reshape(RET_HEADS, rows, rows)
    cross = jnp.broadcast_to(jnp.exp(log_g * (i[None, :, None] + 1.0)), (RET_HEADS, dec, RET_DV)).astype(_F32)
    kdec = jnp.broadcast_to(jnp.tile(jnp.exp(log_g * (dec - 1.0 - i)[None, :, None]), (1, nb, 1)),
                            (RET_HEADS, rows, RET_DK)).astype(_F32)
    sdec = jnp.broadcast_to(jnp.exp(log_g * float(dec)), (RET_HEADS, RET_DK, RET_DV)).astype(_F32)
    w = RET_HEADS * RET_DK
    blk0 = row0 // rows
    row_in = pl.BlockSpec((rows, w), lambda g: (blk0 + g, 0))
    st = pl.BlockSpec((nb, RET_HEADS, RET_DK, RET_DV), lambda g: (g, 0, 0, 0))
    full = lambda a: pl.BlockSpec(a.shape, lambda g: (0,) * a.ndim)
    return pl.pallas_call(
        functools.partial(_ret_sample_kernel, dec=dec),
        grid=(db // nb,),
        in_specs=[row_in, row_in, row_in, st, full(dblk), full(cross), full(kdec), full(sdec)],
        out_specs=(pl.BlockSpec((rows, w), lambda g: (g, 0)), st),
        out_shape=(jax.ShapeDtypeStruct((db * dec, w), _F32),
                   jax.ShapeDtypeStruct((db, RET_HEADS, RET_DK, RET_DV), _F32)),
        compiler_params=pltpu.CompilerParams(dimension_semantics=("parallel",), vmem_limit_bytes=VMEM_LIMIT),
        name="ret_sample",
    )(rq, rk, rv, state, dblk, cross, kdec, sdec)


def _attn_sample_kernel(pt_ref, q_ref, cnew_ref, rnew_ref, *refs, dec):
    npg = SAMPLE_PAGES_PER_STEP
    ckv_refs = refs[:npg]
    kr_refs = refs[npg:2 * npg]
    o_ref, m_ref, l_ref, acc_ref = refs[2 * npg:]
    g = pl.program_id(1)
    rows = MLA_HEADS * dec
    q = q_ref[...].reshape(rows, QK_WIDTH).astype(_BF16)
    q_lat, q_rope = q[:, :KV_LORA], q[:, LANES:LANES + ROPE_DIM]

    @pl.when(g == 0)
    def _():
        m_ref[...] = jnp.full((rows, 1), _NEG_INF, _F32)
        l_ref[...] = jnp.zeros((rows, 1), _F32)
        acc_ref[...] = jnp.zeros((rows, KV_LORA), _F32)

    def update(s_list, v_list):
        s = jnp.concatenate(s_list, axis=1) if len(s_list) > 1 else s_list[0]
        m_prev = m_ref[...]
        m_new = jnp.maximum(m_prev, jnp.max(s, axis=1, keepdims=True))
        alpha = jnp.exp(m_prev - m_new)
        p = jnp.exp(s - m_new)
        l_ref[...] = alpha * l_ref[...] + jnp.sum(p, axis=1, keepdims=True)
        pv = None
        for j, v in enumerate(v_list):
            term = _dot(p[:, j * PAGE_SIZE:(j + 1) * PAGE_SIZE].astype(_BF16), v)
            pv = term if pv is None else pv + term
        acc_ref[...] = alpha * acc_ref[...] + pv
        m_ref[...] = m_new

    s_list, v_list = [], []
    for j in range(npg):
        kc = ckv_refs[j][...].astype(_BF16)
        kr = kr_refs[j][...].astype(_BF16)
        s_list.append(_dot_nt(q_lat, kc) + _dot_nt(q_rope, kr))
        v_list.append(kc)
    update(s_list, v_list)

    @pl.when(g == pl.num_programs(1) - 1)
    def _():
        pad = PAGE_SIZE - dec
        kc = jnp.concatenate([cnew_ref[...], jnp.zeros((pad, KV_LORA), _F32)], axis=0).astype(_BF16)
        kr = jnp.concatenate([rnew_ref[...], jnp.zeros((pad, ROPE_DIM), _F32)], axis=0).astype(_BF16)
        s = _dot_nt(q_lat, kc) + _dot_nt(q_rope, kr)
        qi = lax.broadcasted_iota(jnp.int32, (rows, PAGE_SIZE), 0) & (dec - 1)
        kj = lax.broadcasted_iota(jnp.int32, (rows, PAGE_SIZE), 1)
        s = jnp.where(kj <= qi, s, _NEG_INF)
        update([s], [kc])
        out = acc_ref[...] / l_ref[...]
        for hd in range(MLA_HEADS):
            o_ref[:, hd * KV_LORA:(hd + 1) * KV_LORA] = out[hd * dec:(hd + 1) * dec]


def _attn_sample(q_s, ckv_s, krope_s, cache_ckv, cache_krope, page_table, db, dec):
    npg = SAMPLE_PAGES_PER_STEP
    n_pages = page_table.shape[1]
    ng = n_pages // npg
    rows = MLA_HEADS * dec
    pt_flat = page_table.reshape(-1).astype(jnp.int32)

    def page_spec(width, j):
        return pl.BlockSpec((None, PAGE_SIZE, width),
                            lambda b, g, pt: (pt[b * n_pages + g * npg + j], 0, 0))

    in_specs = [pl.BlockSpec((MLA_HEADS, dec, QK_WIDTH), lambda b, g, pt: (0, b, 0)),
                pl.BlockSpec((dec, KV_LORA), lambda b, g, pt: (b, 0)),
                pl.BlockSpec((dec, ROPE_DIM), lambda b, g, pt: (b, 0))]
    in_specs += [page_spec(KV_LORA, j) for j in range(npg)]
    in_specs += [page_spec(ROPE_DIM, j) for j in range(npg)]
    grid_spec = pltpu.PrefetchScalarGridSpec(
        num_scalar_prefetch=1,
        grid=(db, ng),
        in_specs=in_specs,
        out_specs=pl.BlockSpec((dec, MLA_HEADS * KV_LORA), lambda b, g, pt: (b, 0)),
        scratch_shapes=[pltpu.VMEM((rows, 1), _F32), pltpu.VMEM((rows, 1), _F32),
                        pltpu.VMEM((rows, KV_LORA), _F32)],
    )
    return pl.pallas_call(
        functools.partial(_attn_sample_kernel, dec=dec),
        grid_spec=grid_spec,
        out_shape=jax.ShapeDtypeStruct((db * dec, MLA_HEADS * KV_LORA), _F32),
        compiler_params=pltpu.CompilerParams(dimension_semantics=("parallel", "arbitrary"),
                                             vmem_limit_bytes=VMEM_LIMIT),
        name="attn_sample",
    )(pt_flat, q_s, ckv_s, krope_s, *([cache_ckv] * npg), *([cache_krope] * npg))


def _mixout_kernel(lat_ref, ron_ref, sg_ref, x_ref, wuv_ref, wout_ref, gffn_ref, x1_ref, ht_ref):
    mla = _dot(lat_ref[...], wuv_ref[...])
    ret = ron_ref[...] * sg_ref[...]
    mix = jnp.concatenate([mla, ret], axis=1).astype(_BF16)
    x1 = x_ref[...] + _dot(mix, wout_ref[...])
    x1_ref[...] = x1
    hf = _rms(x1) * gffn_ref[...]
    ht_ref[...] = hf.T.astype(_BF16)


def _mixout(lat, ron, sg, x_all, wuv, wout, gffn):
    tt, d = x_all.shape
    tb = TOKEN_BLOCK
    row = lambda width: pl.BlockSpec((tb, width), lambda i: (i, 0))
    full = lambda a: pl.BlockSpec(a.shape, lambda i: (0,) * a.ndim)
    return pl.pallas_call(
        _mixout_kernel,
        grid=(tt // tb,),
        in_specs=[row(lat.shape[1]), row(ron.shape[1]), row(sg.shape[1]), row(d), full(wuv), full(wout), full(gffn)],
        out_specs=(row(d), pl.BlockSpec((d, tb), lambda i: (0, i))),
        out_shape=(jax.ShapeDtypeStruct((tt, d), _F32), jax.ShapeDtypeStruct((d, tt), _BF16)),
        compiler_params=pltpu.CompilerParams(dimension_semantics=("parallel",), vmem_limit_bytes=VMEM_LIMIT),
        name="mixout",
    )(lat, ron, sg, x_all, wuv, wout, gffn)


_CAND_ROWS = 80


def _top16(x, vals_ref, rank_ref, hd):
    n, tb = x.shape
    iota = lax.broadcasted_iota(jnp.int32, (n, tb), 0)

    def body(r, carry):
        x, rank = carry
        m = jnp.max(x, axis=0, keepdims=True)
        idx = jnp.min(jnp.where(x == m, iota, n), axis=0, keepdims=True)
        hit = iota == idx
        vals_ref[pl.ds(r, 1), :] = m
        return jnp.where(hit, _NEG_INF, x), jnp.where(hit, r.astype(_F32), rank)

    _, rank = lax.fori_loop(0, PEER_TOPK, body, (x, jnp.full((n, tb), float(PEER_TOPK), _F32)))
    rank_ref[hd] = rank


def _peer_select(ht_ref, wqt_ref, sk_ref, qt_ref, s1s_ref, s2s_ref, rank1_ref, rank2_ref, b_ref, n_ref, a_ref):
    tb = ht_ref.shape[1]
    half = PEER_KEY_DIM // 2
    qt_ref[...] = _dot(wqt_ref[...], ht_ref[...]).astype(_BF16)
    rho = lax.broadcasted_iota(jnp.int32, (_CAND_ROWS, tb), 0)
    cand_idx = jnp.where(rho < 16, rho,
                         jnp.where(rho < 72, (((rho - 16) >> 3) + 1) * 16 + ((rho - 16) & 7), (rho - 64) * 16))

    def head(hd, carry):
        base = pl.multiple_of(hd * 2 * half, 2 * half)
        q1 = qt_ref[pl.ds(base, half), :]
        q2 = qt_ref[pl.ds(base + half, half), :]
        sc1 = _dot(sk_ref[hd, 0], q1)
        sc2 = _dot(sk_ref[hd, 1], q2)
        _top16(sc1, s1s_ref, rank1_ref, hd)
        _top16(sc2, s2s_ref, rank2_ref, hd)
        s1s, s2s = s1s_ref[...], s2s_ref[...]
        groups = [s1s[0:1] + s2s]
        groups += [s1s[r:r + 1] + s2s[0:8] for r in range(1, 8)]
        groups += [s1s[8:16] + s2s[0:1]]
        cand0 = jnp.concatenate(groups, axis=0)

        def pick(_, cand):
            m = jnp.max(cand, axis=0, keepdims=True)
            idx = jnp.min(jnp.where(cand == m, cand_idx, 1 << 20), axis=0, keepdims=True)
            return jnp.where(cand_idx == idx, _NEG_INF, cand)

        cand = lax.fori_loop(0, PEER_TOPK, pick, cand0)
        sel = cand != cand0
        zsum = jnp.sum(jnp.where(sel, jnp.exp(cand0 - cand0[0:1]), 0.0), axis=0, keepdims=True)
        self32 = jnp.where(sel, 1.0, 0.0)
        counts = [jnp.sum(self32[0:16], axis=0, keepdims=True)]
        counts += [jnp.sum(self32[16 + 8 * (r - 1):16 + 8 * r], axis=0, keepdims=True) for r in range(1, 8)]
        counts += [self32[72 + r:73 + r] for r in range(8)]
        rank1 = rank1_ref[hd]
        n_by_key = jnp.zeros_like(rank1)
        for r in range(PEER_TOPK):
            n_by_key = jnp.where(rank1 == float(r), counts[r], n_by_key)
        n_ref[hd] = n_by_key
        a_ref[hd] = jnp.exp(sc1 - s1s[0:1]) / zsum
        b_ref[hd] = jnp.exp(sc2 - s2s[0:1])
        return carry

    lax.fori_loop(0, PEER_HEADS, head, 0)


def _peer_kernel(ht_ref, x1_ref, wqt_ref, sk_ref, u_ref, vt_ref, gfin_ref, y_ref,
                 qt_ref, s1s_ref, s2s_ref, rank1_ref, rank2_ref, b_ref, n_ref, a_ref, g_ref, p_ref, acc_ref):
    c = pl.program_id(1)
    sub = PEER_EXPERT_BLOCK // N_KEYS

    @pl.when(c == 0)
    def _():
        _peer_select(ht_ref, wqt_ref, sk_ref, qt_ref, s1s_ref, s2s_ref, rank1_ref, rank2_ref, b_ref, n_ref, a_ref)
        acc_ref[...] = jnp.zeros_like(acc_ref)

    act = _dot(u_ref[...], ht_ref[...])
    g_ref[...] = 0.5 * act * (1.0 + lax.erf(act * math.sqrt(0.5)))

    def sub_block(s, carry):
        i1 = c * sub + s
        w = None
        for hd in range(PEER_HEADS):
            n_row = n_ref[hd, pl.ds(i1, 1), :]
            a_row = a_ref[hd, pl.ds(i1, 1), :]
            term = jnp.where(rank2_ref[hd] < n_row, b_ref[hd] * a_row, 0.0)
            w = term if w is None else w + term
        rows = pl.ds(pl.multiple_of(s * N_KEYS, N_KEYS), N_KEYS)
        p_ref[rows, :] = (g_ref[rows, :] * w).astype(_BF16)
        return carry

    lax.fori_loop(0, sub, sub_block, 0)
    acc_ref[...] += _dot(vt_ref[...], p_ref[...])

    @pl.when(c == pl.num_programs(1) - 1)
    def _():
        x2 = x1_ref[...] + acc_ref[...].T
        y_ref[...] = _rms(x2) * gfin_ref[...]


def _peer(ht, x1, wqt, sk, u_bf, vt_bf, gfin):
    d, tt = ht.shape
    tb, eb = PEER_TOKEN_BLOCK, PEER_EXPERT_BLOCK
    n_exp = u_bf.shape[0]
    full = lambda a: pl.BlockSpec(a.shape, lambda t, c: (0,) * a.ndim)
    per_head = pltpu.VMEM((PEER_HEADS, N_KEYS, tb), _F32)
    return pl.pallas_call(
        _peer_kernel,
        grid=(tt // tb, n_exp // eb),
        in_specs=[pl.BlockSpec((d, tb), lambda t, c: (0, t)),
                  pl.BlockSpec((tb, d), lambda t, c: (t, 0)),
                  full(wqt), full(sk),
                  pl.BlockSpec((eb, d), lambda t, c: (c, 0)),
                  pl.BlockSpec((d, eb), lambda t, c: (0, c)),
                  full(gfin)],
        out_specs=pl.BlockSpec((tb, d), lambda t, c: (t, 0)),
        out_shape=jax.ShapeDtypeStruct((tt, d), _F32),
        scratch_shapes=[pltpu.VMEM((PEER_HEADS * PEER_KEY_DIM, tb), _BF16),
                        pltpu.VMEM((PEER_TOPK, tb), _F32), pltpu.VMEM((PEER_TOPK, tb), _F32),
                        per_head, per_head, per_head, per_head, per_head,
                        pltpu.VMEM((eb, tb), _F32), pltpu.VMEM((eb, tb), _BF16),
                        pltpu.VMEM((d, tb), _F32)],
        compiler_params=pltpu.CompilerParams(dimension_semantics=("parallel", "arbitrary"),
                                             vmem_limit_bytes=VMEM_LIMIT),
        name="peer",
    )(ht, x1, wqt, sk, u_bf, vt_bf, gfin)


def _rope_tables(pos, group):
    half = group // 2
    inv = ROPE_THETA ** (-jnp.arange(half, dtype=_F32) / half)
    ang = pos.astype(_F32)[:, None] * inv[None, :]
    lane = np.arange(LANES)
    freq = (lane % group) % half
    sign = np.where((lane % group) < half, -1.0, 1.0).astype(np.float32)
    return jnp.cos(ang)[:, freq], jnp.sin(ang)[:, freq] * sign[None, :]


def _round_up(n, m):
    return -(-n // m) * m


def kernel(x_prompt, x_sample, cache_ckv, cache_krope, state_ret, page_table, meta_tokens, g_mix, w_in, g_q, w_uq,
           g_kv, w_uk, w_uv, w_out, g_ffn, w_query, sub_keys, u_experts, v_experts, g_final):
    assert w_in.shape[0] == 1, "single-layer step"
    n_batch, seq, d = x_prompt.shape
    db, dec, _ = x_sample.shape
    n_pages = page_table.shape[1]
    past = n_pages * PAGE_SIZE
    l_real = seq + N_META
    lp = _round_up(l_real, ATTN_TILE)
    s0 = n_batch * lp
    tt = _round_up(s0 + db * dec, PEER_TOKEN_BLOCK)
    assert n_pages % SAMPLE_PAGES_PER_STEP == 0 and db % SAMPLE_RET_BATCH == 0 and dec & (dec - 1) == 0

    meta = jnp.broadcast_to(meta_tokens.astype(x_prompt.dtype)[None], (n_batch, N_META, d))
    xp = jnp.concatenate([meta, x_prompt, jnp.zeros((n_batch, lp - l_real, d), x_prompt.dtype)], axis=1)
    x_all = jnp.concatenate([xp.reshape(s0, d), x_sample.reshape(db * dec, d),
                             jnp.zeros((tt - s0 - db * dec, d), x_prompt.dtype)], axis=0)
    pos = jnp.concatenate([jnp.tile(jnp.arange(lp), n_batch), jnp.tile(past + jnp.arange(dec), db),
                           jnp.zeros((tt - s0 - db * dec,), jnp.int32)])
    tabs = _rope_tables(pos, ROPE_DIM) + _rope_tables(pos, RET_DK)

    row2 = lambda g: g.reshape(1, -1).astype(_F32)
    wi = w_in[0]
    n1 = Q_LORA + KV_LORA + ROPE_DIM
    w1 = jnp.concatenate([wi[:, :n1], jnp.zeros((d, 4 * LANES - n1), wi.dtype)], axis=1).astype(_BF16)
    w2 = wi[:, n1:].astype(_BF16)
    wqn = w_uq[0][:, :, :NOPE_DIM].reshape(Q_LORA, MLA_HEADS * NOPE_DIM).astype(_BF16)
    wqr = jnp.concatenate([w_uq[0][:, :, NOPE_DIM:], jnp.zeros((Q_LORA, MLA_HEADS, LANES - ROPE_DIM), w_uq.dtype)],
                          axis=2).reshape(Q_LORA, MLA_HEADS * LANES).astype(_BF16)
    eye = jnp.eye(MLA_HEADS, dtype=w_uk.dtype)
    wuk = jnp.einsum("chd,hg->hdgc", w_uk[0], eye).reshape(MLA_HEADS * NOPE_DIM, MLA_HEADS * KV_LORA).astype(_BF16)
    wuv = jnp.einsum("chd,hg->hcgd", w_uv[0], eye).reshape(MLA_HEADS * KV_LORA, MLA_HEADS * V_DIM).astype(_BF16)
    wout = w_out[0].astype(_BF16)
    wqt = w_query[0].T.astype(_BF16)
    sk = sub_keys[0].astype(_BF16)
    u_bf = u_experts[0].astype(_BF16)
    vt_bf = v_experts[0].T.astype(_BF16)

    qcat, kcat, ckv, krope, rq, rk, rv, sg = _inproj(x_all, tabs, row2(g_mix[0]), w1, w2, row2(g_q[0]), wqn, wqr,
                                                      wuk, row2(g_kv[0]))

    lat_p = _attn_prompt(qcat, kcat, n_batch, lp)
    ron_p, s_p = _ret_prompt(rq, rk, rv, n_batch, lp, l_real)

    rows_s = slice(s0, s0 + db * dec)
    ckv_s, krope_s = ckv[rows_s], krope[rows_s]
    q_s = qcat[:, rows_s].astype(_F32)
    lat_s = _attn_sample(q_s, ckv_s, krope_s, cache_ckv[0], cache_krope[0], page_table, db, dec)
    ron_s, s_s = _ret_sample(rq, rk, rv, state_ret[0], s0, db, dec)

    tail = tt - s0 - db * dec
    lat = jnp.concatenate([lat_p, lat_s.astype(_BF16), jnp.zeros((tail, lat_p.shape[1]), _BF16)], axis=0)
    ron = jnp.concatenate([ron_p, ron_s, jnp.zeros((tail, ron_p.shape[1]), _F32)], axis=0)

    x1, ht = _mixout(lat, ron, sg, x_all, wuv, wout, row2(g_ffn[0]))
    y = _peer(ht, x1, wqt, sk, u_bf, vt_bf, row2(g_final))

    per_batch = lambda a: a[:s0].reshape(n_batch, lp, a.shape[-1])
    y_prompt = per_batch(y)[:, N_META:l_real]
    y_sample = y[rows_s].reshape(db, dec, d)
    new_ckv_prompt = per_batch(ckv)[None, :, :l_real]
    new_krope_prompt = per_batch(krope)[None, :, :l_real]
    new_ckv_sample = ckv_s.reshape(1, db, dec, KV_LORA)
    new_krope_sample = krope_s.reshape(1, db, dec, ROPE_DIM)
    return (y_prompt, y_sample, new_ckv_prompt, new_krope_prompt, new_ckv_sample, new_krope_sample,
            s_p[None], s_s[None])
```
